```python
import jax, jax.numpy as jnp
from jax import lax
import numpy as np


D_MODEL = 4096
BATCH = 1
SEQ = 16384
DEPTH = 2

N_GROUPS = 4
GROUP_WIDTH = D_MODEL // N_GROUPS
MIX_WIDTH = N_GROUPS * GROUP_WIDTH
CHUNK = 64
EPS = 1e-6
GLA_HEADS = 4
GLA_DV = GROUP_WIDTH // GLA_HEADS
GLA_DK = GLA_DV // 2
GLA_LOWRANK = 16
GLA_GATE_NORMALIZER = 16.0
RG_BLOCKS = 8
RG_BLOCK = GROUP_WIDTH // RG_BLOCKS
RG_CONV = 4
RG_C = 8.0
RG_A_MIN = 0.9
RG_A_MAX = 0.999
ML_HEADS = 4
ML_DV = GROUP_WIDTH // ML_HEADS
ML_DK = ML_DV // 2
ML_FGATE_BIAS = 3.0
ML_IGATE_BIAS = -2.0
HG_HEADS = 4
HG_DV = GROUP_WIDTH // HG_HEADS
HG_DK = HG_DV // 2
D_FF = 11008
N_EXPERTS = 8
TOP_K = 2
D_FF_EXPERT = D_FF // 2
N_DENSE = (DEPTH + 1) // 2
N_MOE = DEPTH // 2
PLE_DIM = 256

SPLIT_SIZES = (
    GLA_HEADS * GLA_DK, GLA_HEADS * GLA_DK, GROUP_WIDTH, GROUP_WIDTH, GLA_LOWRANK,
    GROUP_WIDTH, GROUP_WIDTH,
    ML_HEADS * ML_DK, ML_HEADS * ML_DK, GROUP_WIDTH, GROUP_WIDTH, ML_HEADS, ML_HEADS,
    HG_HEADS * HG_DK, HG_HEADS * HG_DK, GROUP_WIDTH, GROUP_WIDTH,
)
PROJ_WIDTH = sum(SPLIT_SIZES)

kernel_name = 'hymba_style_gla_rglru_mlstm_hgrn2_moe'


def rmsnorm(x, gain):
    xf = x.astype(jnp.float32)
    y = xf * lax.rsqrt(jnp.mean(xf * xf, axis=-1, keepdims=True) + EPS)
    return (y * gain.astype(jnp.float32)).astype(x.dtype)


def heads(t, n):
    return t.reshape(t.shape[:-1] + (n, t.shape[-1] // n))


def to_chunks(t):
    b, s = t.shape[:2]
    t = t.reshape((b, s // CHUNK, CHUNK) + t.shape[2:])
    return jnp.swapaxes(jnp.moveaxis(t, 1, 0), 2, 3)


def from_chunks(t):
    t = jnp.moveaxis(jnp.swapaxes(t, 2, 3), 0, 1)
    return t.reshape((t.shape[0], t.shape[1] * t.shape[2]) + t.shape[3:])


def chunk_gla(q, k, v, log_a):
    f32 = jnp.float32
    q, k, v, log_a = (t.astype(f32) for t in (q, k, v, log_a))
    bsz, _, nh, dk = q.shape
    dv = v.shape[-1]
    causal = jnp.tril(jnp.ones((CHUNK, CHUNK), bool))

    def step(state, inp):
        qi, ki, vi, ai = inp
        b = jnp.cumsum(ai, axis=-2)
        diff = b[..., :, None, :] - b[..., None, :, :]
        decay = jnp.exp(jnp.where(causal[:, :, None], diff, -jnp.inf))
        scores = jnp.einsum('bhtd,bhsd,bhtsd->bhts', qi, ki, decay)
        o = (jnp.einsum('bhts,bhsv->bhtv', scores, vi)
             + jnp.einsum('bhtd,bhdv->bhtv', qi * jnp.exp(b), state))
        b_last = b[..., -1:, :]
        k_dec = ki * jnp.exp(b_last - b)
        state = (state * jnp.exp(b[..., -1, :])[..., None]
                 + jnp.einsum('bhsd,bhsv->bhdv', k_dec, vi))
        return state, o

    s0 = jnp.zeros((bsz, nh, dk, dv), f32)
    _, o = lax.scan(step, s0, tuple(to_chunks(t) for t in (q, k, v, log_a)))
    return from_chunks(o)


def chunk_mlstm(q, k, v, i_pre, log_f):
    f32 = jnp.float32
    q, k, v, i_pre, log_f = (t.astype(f32) for t in (q, k, v, i_pre, log_f))
    bsz, _, nh, dk = q.shape
    dv = v.shape[-1]
    causal = jnp.tril(jnp.ones((CHUNK, CHUNK), bool))

    def step(carry, inp):
        c_st, n_st, m_st = carry
        qi, ki, vi, ii, fi = inp
        b = jnp.cumsum(fi, axis=-1)
        dlog = jnp.where(causal, b[..., :, None] - b[..., None, :] + ii[..., None, :], -jnp.inf)
        g = b + m_st[..., None]
        m_t = jnp.maximum(g, jnp.max(dlog, axis=-1))
        w = jnp.exp(dlog - m_t[..., None])
        inter = jnp.exp(g - m_t)
        qk = jnp.einsum('bhtd,bhsd->bhts', qi, ki) * w
        num = (jnp.einsum('bhts,bhsv->bhtv', qk, vi)
               + inter[..., None] * jnp.einsum('bhtd,bhdv->bhtv', qi, c_st))
        den = jnp.sum(qk, axis=-1) + inter * jnp.einsum('bhtd,bhd->bht', qi, n_st)
        h = num / jnp.maximum(jnp.abs(den), jnp.exp(-m_t))[..., None]
        b_last = b[..., -1]
        g_last = b_last + m_st
        d_last = b_last[..., None] - b + ii
        m_new = jnp.maximum(g_last, jnp.max(d_last, axis=-1))
        ws = jnp.exp(d_last - m_new[..., None])
        sc = jnp.exp(g_last - m_new)
        c_new = sc[..., None, None] * c_st + jnp.einsum('bhs,bhsd,bhsv->bhdv', ws, ki, vi)
        n_new = sc[..., None] * n_st + jnp.einsum('bhs,bhsd->bhd', ws, ki)
        return (c_new, n_new, m_new), h

    init = (jnp.zeros((bsz, nh, dk, dv), f32), jnp.zeros((bsz, nh, dk), f32), jnp.zeros((bsz, nh), f32))
    _, h = lax.scan(step, init, tuple(to_chunks(t) for t in (q, k, v, i_pre, log_f)))
    return from_chunks(h)


def gla_group(q, k, v, g, lr, w_up, b_up, norm_gain):
    f32 = jnp.float32
    log_a = jax.nn.log_sigmoid((lr @ w_up + b_up).astype(f32)) / GLA_GATE_NORMALIZER
    o = chunk_gla(heads(q, GLA_HEADS) * (GLA_DK ** -0.5), heads(k, GLA_HEADS),
                  heads(v, GLA_HEADS), heads(log_a, GLA_HEADS))
    o = rmsnorm(o, norm_gain) * jax.nn.silu(heads(g, GLA_HEADS).astype(f32))
    return o.reshape(o.shape[:2] + (GROUP_WIDTH,))


def rglru_group(xb, gate, conv_w, conv_b, w_a, b_a, w_x, b_x, lam):
    f32 = jnp.float32
    xc = lax.conv_general_dilated(xb, conv_w[:, None, :].astype(xb.dtype), window_strides=(1,),
                                  padding=[(RG_CONV - 1, 0)], dimension_numbers=('NWC', 'WIO', 'NWC'),
                                  feature_group_count=GROUP_WIDTH) + conv_b
    xh = heads(xc, RG_BLOCKS)
    r = jax.nn.sigmoid((jnp.einsum('bsnc,ncd->bsnd', xh, w_a).reshape(xc.shape) + b_a).astype(f32))
    i = jax.nn.sigmoid((jnp.einsum('bsnc,ncd->bsnd', xh, w_x).reshape(xc.shape) + b_x).astype(f32))
    log_a = -RG_C * r * jax.nn.softplus(-lam.astype(f32))
    a = jnp.exp(log_a)
    u = jnp.sqrt(-jnp.expm1(2.0 * log_a)) * (i * xc.astype(f32))

    def combine(c1, c2):
        a1, b1 = c1
        a2, b2 = c2
        return a1 * a2, a2 * b1 + b2

    _, h = lax.associative_scan(combine, (a, u), axis=1)
    return jax.nn.gelu(gate.astype(f32)) * h


def mlstm_group(q, k, v, o, i_pre, f_pre, b_i, b_f, norm_gain):
    f32 = jnp.float32
    i_t = (i_pre + b_i).astype(f32)
    log_f = jax.nn.log_sigmoid((f_pre + b_f).astype(f32))
    h = chunk_mlstm(heads(q, ML_HEADS) * (ML_DK ** -0.5), heads(k, ML_HEADS), heads(v, ML_HEADS), i_t, log_f)
    h = rmsnorm(h, heads(norm_gain, ML_HEADS)) * jax.nn.sigmoid(heads(o, ML_HEADS).astype(f32))
    return h.reshape(h.shape[:2] + (GROUP_WIDTH,))


def hgrn2_group(q, f_pre, i, g, lb, norm_gain):
    f32 = jnp.float32
    fp = f_pre.astype(f32)
    log_f = jnp.logaddexp(jnp.log(lb), jnp.log1p(-lb) + jax.nn.log_sigmoid(fp))
    k = (1.0 - lb) * jax.nn.sigmoid(-fp)
    o = chunk_gla(heads(jax.nn.silu(q.astype(f32)), HG_HEADS) * (HG_DK ** -0.5), heads(k, HG_HEADS),
                  heads(i, HG_HEADS), heads(log_f, HG_HEADS))
    o = rmsnorm(o, norm_gain) * jax.nn.silu(heads(g, HG_HEADS).astype(f32))
    return o.reshape(o.shape[:2] + (GROUP_WIDTH,))


def swiglu(x, w1, w3, w2):
    return (jax.nn.silu(x @ w1) * (x @ w3)) @ w2


def moe_top2(x, router, w1, w3, w2):
    f32 = jnp.float32
    logits = (x @ router).astype(f32)
    vals, idx = lax.top_k(logits, TOP_K)
    gates = jax.nn.softmax(vals, axis=-1)
    combine = jnp.sum(jax.nn.one_hot(idx, N_EXPERTS, dtype=f32) * gates[..., None], axis=-2)
    out = jnp.zeros(x.shape, f32)
    for e in range(N_EXPERTS):
        out = out + combine[..., e:e + 1] * swiglu(x, w1[e], w3[e], w2[e]).astype(f32)
    return out.astype(x.dtype)


def _normal(key, shape, scale):
    return jax.random.normal(key, shape, jnp.float32) * scale


def setup_inputs(seed: int = 0) -> dict:
    key = jax.random.key(seed)
    ks = jax.random.split(key, 32)
    G = GROUP_WIDTH
    u = jax.random.uniform(ks[12], (DEPTH, G), jnp.float32, RG_A_MIN, RG_A_MAX)
    s = u ** (1.0 / RG_C)
    rg_lambda = jnp.log(s) - jnp.log1p(-s)
    return {
        'x': _normal(ks[0], (BATCH, SEQ, D_MODEL), 1.0),
        'p': _normal(ks[1], (DEPTH, BATCH, SEQ, PLE_DIM), 1.0),
        'attn_norm': 1.0 + _normal(ks[2], (DEPTH, D_MODEL), 0.05),
        'w_in': _normal(ks[3], (DEPTH, D_MODEL, PROJ_WIDTH), D_MODEL ** -0.5),
        'w_out': _normal(ks[4], (DEPTH, MIX_WIDTH, D_MODEL), MIX_WIDTH ** -0.5),
        'gla_w_up': _normal(ks[5], (DEPTH, GLA_LOWRANK, GLA_HEADS * GLA_DK), GLA_LOWRANK ** -0.5),
        'gla_b_up': _normal(ks[6], (DEPTH, GLA_HEADS * GLA_DK), 0.1),
        'gla_norm': 1.0 + _normal(ks[7], (DEPTH, GLA_DV), 0.05),
        'rg_conv_w': _normal(ks[8], (DEPTH, RG_CONV, G), RG_CONV ** -0.5),
        'rg_conv_b': _normal(ks[9], (DEPTH, G), 0.02),
        'rg_w_a': _normal(ks[10], (DEPTH, RG_BLOCKS, RG_BLOCK, RG_BLOCK), RG_BLOCK ** -0.5),
        'rg_b_a': _normal(ks[11], (DEPTH, G), 0.1),
        'rg_w_x': _normal(ks[13], (DEPTH, RG_BLOCKS, RG_BLOCK, RG_BLOCK), RG_BLOCK ** -0.5),
        'rg_b_x': _normal(ks[14], (DEPTH, G), 0.1),
        'rg_lambda': rg_lambda,
        'ml_b_i': ML_IGATE_BIAS + _normal(ks[15], (DEPTH, ML_HEADS), 0.1),
        'ml_b_f': ML_FGATE_BIAS + _normal(ks[16], (DEPTH, ML_HEADS), 0.1),
        'ml_norm': 1.0 + _normal(ks[17], (DEPTH, G), 0.05),
        'hg_lb_logits': _normal(ks[18], (DEPTH, HG_HEADS * HG_DK), 1.0),
        'hg_norm': 1.0 + _normal(ks[19], (DEPTH, HG_DV), 0.05),
        'ffn_norm': 1.0 + _normal(ks[20], (DEPTH, D_MODEL), 0.05),
        'ffn_w1': _normal(ks[21], (N_DENSE, D_MODEL, D_FF), D_MODEL ** -0.5),
        'ffn_w3': _normal(ks[22], (N_DENSE, D_MODEL, D_FF), D_MODEL ** -0.5),
        'ffn_w2': _normal(ks[23], (N_DENSE, D_FF, D_MODEL), D_FF ** -0.5),
        'moe_router': _normal(ks[24], (N_MOE, D_MODEL, N_EXPERTS), D_MODEL ** -0.5),
        'moe_w1': _normal(ks[25], (N_MOE, N_EXPERTS, D_MODEL, D_FF_EXPERT), D_MODEL ** -0.5),
        'moe_w3': _normal(ks[26], (N_MOE, N_EXPERTS, D_MODEL, D_FF_EXPERT), D_MODEL ** -0.5),
        'moe_w2': _normal(ks[27], (N_MOE, N_EXPERTS, D_FF_EXPERT, D_MODEL), D_FF_EXPERT ** -0.5),
        'ple_norm': 1.0 + _normal(ks[28], (DEPTH, D_MODEL), 0.05),
        'ple_w_gate': _normal(ks[29], (DEPTH, D_MODEL, D_MODEL), D_MODEL ** -0.5),
        'ple_w_proj': _normal(ks[30], (DEPTH, PLE_DIM, D_MODEL), PLE_DIM ** -0.5),
        'final_norm': 1.0 + _normal(ks[31], (D_MODEL,), 0.05),
    }


def reference(x, p, attn_norm, w_in, w_out, gla_w_up, gla_b_up, gla_norm, rg_conv_w, rg_conv_b,
              rg_w_a, rg_b_a, rg_w_x, rg_b_x, rg_lambda, ml_b_i, ml_b_f, ml_norm, hg_lb_logits, hg_norm,
              ffn_norm, ffn_w1, ffn_w3, ffn_w2, moe_router, moe_w1, moe_w3, moe_w2,
              ple_norm, ple_w_gate, ple_w_proj, final_norm):
    split_at = [int(c) for c in np.cumsum(SPLIT_SIZES)[:-1]]
    sm = jax.nn.softmax(hg_lb_logits.astype(jnp.float32), axis=0)
    lb_all = jnp.cumsum(jnp.where(jnp.arange(DEPTH)[:, None] > 0, sm, 0.0), axis=0)
    h = x
    for l in range(DEPTH):
        hn = rmsnorm(h, attn_norm[l])
        (ga_q, ga_k, ga_v, ga_g, ga_lr, rg_x, rg_gate, ml_q, ml_k, ml_v, ml_o, ml_i, ml_f,
         hg_q, hg_f, hg_i, hg_g) = jnp.split(hn @ w_in[l], split_at, axis=-1)
        y_a = gla_group(ga_q, ga_k, ga_v, ga_g, ga_lr, gla_w_up[l], gla_b_up[l], gla_norm[l])
        y_b = rglru_group(rg_x, rg_gate, rg_conv_w[l], rg_conv_b[l], rg_w_a[l], rg_b_a[l],
                          rg_w_x[l], rg_b_x[l], rg_lambda[l])
        y_c = mlstm_group(ml_q, ml_k, ml_v, ml_o, ml_i, ml_f, ml_b_i[l], ml_b_f[l], ml_norm[l])
        y_d = hgrn2_group(hg_q, hg_f, hg_i, hg_g, lb_all[l], hg_norm[l])
        mixed = jnp.concatenate([y_a, y_b, y_c, y_d], axis=-1).astype(h.dtype)
        h = h + mixed @ w_out[l]
        hn = rmsnorm(h, ffn_norm[l])
        j = l // 2
        if l % 2 == 0:
            h = h + swiglu(hn, ffn_w1[j], ffn_w3[j], ffn_w2[j])
        else:
            h = h + moe_top2(hn, moe_router[j], moe_w1[j], moe_w3[j], moe_w2[j])
        gate = jax.nn.sigmoid((rmsnorm(h, ple_norm[l]) @ ple_w_gate[l]).astype(jnp.float32))
        h = h + (gate * (p[l] @ ple_w_proj[l]).astype(jnp.float32)).astype(h.dtype)
    return rmsnorm(h, final_norm)
```

```python
import functools

import numpy as np
import jax
import jax.numpy as jnp
from jax import lax
from jax.experimental import pallas as pl
from jax.experimental.pallas import tpu as pltpu

F32 = jnp.float32
BF16 = jnp.bfloat16

D_MODEL = 4096
SEQ = 16384
DEPTH = 2
GROUP_WIDTH = 1024
EPS = 1e-6
N_HEADS = 4
DK = 128
DV = 256
GLA_LOWRANK = 16
GLA_GATE_NORMALIZER = 16.0
RG_BLOCKS = 8
RG_BLOCK = 128
RG_CONV = 4
RG_C = 8.0
D_FF = 11008
D_FF_PAD = 11264
N_EXPERTS = 8
D_FF_EXPERT = 5504
D_FF_EXPERT_PAD = 5632
PLE_DIM = 256
LANES = 128
SMALL_W = LANES
VMEM_LIMIT = 56 * 1024 * 1024

OFF_GA_Q, OFF_GA_K, OFF_GA_V, OFF_GA_G = 0, 512, 1024, 2048
OFF_RG_X, OFF_RG_GATE = 3072, 4096
OFF_ML_Q, OFF_ML_K, OFF_ML_V, OFF_ML_O = 5120, 5632, 6144, 7168
OFF_HG_Q, OFF_HG_F, OFF_HG_I, OFF_HG_G = 8192, 8704, 9216, 10240
MAIN_W = 11264
SM_LR, SM_ML_I, SM_ML_F = 0, 16, 20


def _params(sem, vmem=VMEM_LIMIT):
    return pltpu.CompilerParams(dimension_semantics=sem, vmem_limit_bytes=vmem)


def _dot(a, b):
    return jnp.dot(a, b, preferred_element_type=F32)


def _dot_nt(a, b):
    return lax.dot_general(a, b, (((1,), (1,)), ((), ())), preferred_element_type=F32)


def _dot_tn(a, b):
    return lax.dot_general(a, b, (((0,), (0,)), ((), ())), preferred_element_type=F32)


def _log_sigmoid(x):
    return jnp.minimum(x, 0.0) - jnp.log1p(jnp.exp(-jnp.abs(x)))


def _sigmoid(x):
    return jax.nn.sigmoid(x)


def _silu(x):
    return x * jax.nn.sigmoid(x)


def _cumsum_rows(x, n):
    row = lax.broadcasted_iota(jnp.int32, x.shape, 0)
    s = 1
    while s < n:
        x = x + jnp.where(row >= s, pltpu.roll(x, s, 0), 0.0)
        s *= 2
    return x


def _rmsnorm_kernel(x_ref, g_ref, o_ref):
    x = x_ref[...]
    ms = jnp.mean(x * x, axis=-1, keepdims=True)
    o_ref[...] = (x * lax.rsqrt(ms + EPS) * g_ref[...]).astype(o_ref.dtype)


def rmsnorm(x, gain, out_dtype, tm=256):
    s, d = x.shape
    return pl.pallas_call(
        _rmsnorm_kernel, grid=(s // tm,),
        in_specs=[pl.BlockSpec((tm, d), lambda i: (i, 0)), pl.BlockSpec((1, d), lambda i: (0, 0))],
        out_specs=pl.BlockSpec((tm, d), lambda i: (i, 0)),
        out_shape=jax.ShapeDtypeStruct((s, d), out_dtype),
        compiler_params=_params(("parallel",)), name="rmsnorm",
    )(x, gain.reshape(1, d))


def _rmsnorm_router_kernel(x_ref, g_ref, r_ref, o_ref, c_ref):
    x = x_ref[...]
    ms = jnp.mean(x * x, axis=-1, keepdims=True)
    y = x * lax.rsqrt(ms + EPS) * g_ref[...]
    o_ref[...] = y.astype(o_ref.dtype)
    logits = jnp.dot(y, r_ref[...], preferred_element_type=F32, precision=lax.Precision.HIGHEST)
    lane = lax.broadcasted_iota(jnp.int32, logits.shape, 1)
    neg = jnp.float32(-jnp.inf)
    lg = jnp.where(lane < N_EXPERTS, logits, neg)
    m1 = jnp.max(lg, axis=-1, keepdims=True)
    i1 = jnp.min(jnp.where(lg == m1, lane, LANES), axis=-1, keepdims=True)
    sel1 = lane == i1
    lg2 = jnp.where(sel1, neg, lg)
    m2 = jnp.max(lg2, axis=-1, keepdims=True)
    i2 = jnp.min(jnp.where(lg2 == m2, lane, LANES), axis=-1, keepdims=True)
    sel2 = lane == i2
    e2 = jnp.exp(m2 - m1)
    g1 = 1.0 / (1.0 + e2)
    g2 = e2 / (1.0 + e2)
    c_ref[...] = jnp.where(sel1, g1, 0.0) + jnp.where(sel2, g2, 0.0)


def rmsnorm_router(x, gain, router, tm=256):
    s, d = x.shape
    r = jnp.pad(router.astype(F32), ((0, 0), (0, LANES - N_EXPERTS)))
    return pl.pallas_call(
        _rmsnorm_router_kernel, grid=(s // tm,),
        in_specs=[pl.BlockSpec((tm, d), lambda i: (i, 0)), pl.BlockSpec((1, d), lambda i: (0, 0)),
                  pl.BlockSpec((d, LANES), lambda i: (0, 0))],
        out_specs=[pl.BlockSpec((tm, d), lambda i: (i, 0)), pl.BlockSpec((tm, LANES), lambda i: (i, 0))],
        out_shape=[jax.ShapeDtypeStruct((s, d), BF16), jax.ShapeDtypeStruct((s, LANES), F32)],
        compiler_params=_params(("parallel",)), name="rmsnorm_router",
    )(x, gain.reshape(1, d), r)


def _mm_plain_kernel(a_ref, w_ref, o_ref):
    o_ref[...] = _dot(a_ref[...], w_ref[...]).astype(o_ref.dtype)


def mm_plain(a, w, out_dtype, tm=1024, tn=1024):
    m, k = a.shape
    n = w.shape[1]
    return pl.pallas_call(
        _mm_plain_kernel, grid=(n // tn, m // tm),
        in_specs=[pl.BlockSpec((tm, k), lambda j, i: (i, 0)), pl.BlockSpec((k, tn), lambda j, i: (0, j))],
        out_specs=pl.BlockSpec((tm, tn), lambda j, i: (i, j)),
        out_shape=jax.ShapeDtypeStruct((m, n), out_dtype),
        compiler_params=_params(("parallel", "parallel")), name="mm_plain",
    )(a, w)


def _small_proj_kernel(a_ref, w_ref, wt_ref, o_ref, ot_ref):
    a = a_ref[...]
    o_ref[...] = _dot(a, w_ref[...])
    ot_ref[...] = _dot_nt(wt_ref[...], a)


def small_proj(a, w_small, tm=1024):
    m, k = a.shape
    return pl.pallas_call(
        _small_proj_kernel, grid=(m // tm,),
        in_specs=[pl.BlockSpec((tm, k), lambda i: (i, 0)), pl.BlockSpec((k, SMALL_W), lambda i: (0, 0)),
                  pl.BlockSpec((SMALL_W, k), lambda i: (0, 0))],
        out_specs=[pl.BlockSpec((tm, SMALL_W), lambda i: (i, 0)), pl.BlockSpec((SMALL_W, tm), lambda i: (0, i))],
        out_shape=[jax.ShapeDtypeStruct((m, SMALL_W), F32), jax.ShapeDtypeStruct((SMALL_W, m), F32)],
        compiler_params=_params(("parallel",)), name="small_proj",
    )(a, w_small, w_small.T)


def _mm_out_kernel(a0_ref, a1_ref, a2_ref, a3_ref, w_ref, r_ref, o_ref):
    acc = r_ref[...]
    for g, a_ref in enumerate((a0_ref, a1_ref, a2_ref, a3_ref)):
        acc = acc + _dot(a_ref[...], w_ref[g * GROUP_WIDTH:(g + 1) * GROUP_WIDTH, :])
    o_ref[...] = acc


def mm_out(ys, w, res, tm=512, tn=1024):
    m = res.shape[0]
    k, n = w.shape
    a_spec = pl.BlockSpec((tm, GROUP_WIDTH), lambda j, i: (i, 0))
    return pl.pallas_call(
        _mm_out_kernel, grid=(n // tn, m // tm),
        in_specs=[a_spec, a_spec, a_spec, a_spec, pl.BlockSpec((k, tn), lambda j, i: (0, j)),
                  pl.BlockSpec((tm, tn), lambda j, i: (i, j))],
        out_specs=pl.BlockSpec((tm, tn), lambda j, i: (i, j)),
        out_shape=jax.ShapeDtypeStruct((m, n), F32),
        compiler_params=_params(("parallel", "parallel")), name="mm_out",
    )(*ys, w, res)


def _swiglu_up_kernel(a_ref, w1_ref, w3_ref, o_ref):
    a = a_ref[...]
    o_ref[...] = (_silu(_dot(a, w1_ref[...])) * _dot(a, w3_ref[...])).astype(o_ref.dtype)


def swiglu_up(a, w1, w3, tm=1024, tn=512):
    m, k = a.shape
    n = w1.shape[1]
    w_spec = pl.BlockSpec((k, tn), lambda j, i: (0, j))
    return pl.pallas_call(
        _swiglu_up_kernel, grid=(n // tn, m // tm),
        in_specs=[pl.BlockSpec((tm, k), lambda j, i: (i, 0)), w_spec, w_spec],
        out_specs=pl.BlockSpec((tm, tn), lambda j, i: (i, j)),
        out_shape=jax.ShapeDtypeStruct((m, n), BF16),
        compiler_params=_params(("parallel", "parallel")), name="swiglu_up",
    )(a, w1, w3)


def _mm_res_kernel(*refs, nk, scaled):
    if scaled:
        a_ref, w_ref, r_ref, s_ref, o_ref, acc_ref = refs
    else:
        a_ref, w_ref, r_ref, o_ref, acc_ref = refs
    kk = pl.program_id(2)

    @pl.when(kk == 0)
    def _():
        acc_ref[...] = jnp.zeros_like(acc_ref)

    acc_ref[...] += _dot(a_ref[...], w_ref[...])

    @pl.when(kk == nk - 1)
    def _():
        acc = acc_ref[...]
        if scaled:
            acc = acc * s_ref[...]
        o_ref[...] = r_ref[...] + acc


def mm_res(a, w, res, scale=None, tm=1024, tn=1024, tk=2816):
    m, k = a.shape
    n = w.shape[1]
    nk = k // tk
    in_specs = [pl.BlockSpec((tm, tk), lambda i, j, kk: (i, kk)), pl.BlockSpec((tk, tn), lambda i, j, kk: (kk, j)),
                pl.BlockSpec((tm, tn), lambda i, j, kk: (i, j))]
    args = [a, w, res]
    if scale is not None:
        in_specs.append(pl.BlockSpec((tm, 1), lambda i, j, kk: (i, 0)))
        args.append(scale)
    return pl.pallas_call(
        functools.partial(_mm_res_kernel, nk=nk, scaled=scale is not None), grid=(m // tm, n // tn, nk),
        in_specs=in_specs, out_specs=pl.BlockSpec((tm, tn), lambda i, j, kk: (i, j)),
        out_shape=jax.ShapeDtypeStruct((m, n), F32),
        scratch_shapes=[pltpu.VMEM((tm, tn), F32)],
        compiler_params=_params(("parallel", "parallel", "arbitrary")), name="mm_res",
    )(*args)


def _ple_kernel(a_ref, wg_ref, p_ref, wp_ref, r_ref, o_ref):
    gate = _sigmoid(_dot(a_ref[...], wg_ref[...]))
    emb = _dot(p_ref[...].astype(BF16), wp_ref[...])
    o_ref[...] = r_ref[...] + gate * emb


def ple(a, wg, p, wp, res, tm=512, tn=1024):
    m, k = a.shape
    n = wg.shape[1]
    kp = p.shape[1]
    return pl.pallas_call(
        _ple_kernel, grid=(n // tn, m // tm),
        in_specs=[pl.BlockSpec((tm, k), lambda j, i: (i, 0)), pl.BlockSpec((k, tn), lambda j, i: (0, j)),
                  pl.BlockSpec((tm, kp), lambda j, i: (i, 0)), pl.BlockSpec((kp, tn), lambda j, i: (0, j)),
                  pl.BlockSpec((tm, tn), lambda j, i: (i, j))],
        out_specs=pl.BlockSpec((tm, tn), lambda j, i: (i, j)),
        out_shape=jax.ShapeDtypeStruct((m, n), F32),
        compiler_params=_params(("parallel", "parallel")), name="ple",
    )(a, wg, p, wp, res)


GLA_T = 128
GLA_SUB = 8


def _gla_codes(t_blk, sub):
    t = np.arange(t_blk)[:, None]
    s = np.arange(t_blk)[None, :]
    code = np.full((t_blk, t_blk), -1, np.int32)
    band = ((t // sub) == (s // sub)) & (t >= s)
    code[band] = (t - s)[band]
    lvl, blk = 0, t_blk
    while blk > sub:
        half = blk // 2
        m = ((t // blk) == (s // blk)) & ((t % blk) >= half) & ((s % blk) < half)
        code[m] = sub + lvl
        lvl += 1
        blk //= 2
    return code, lvl


def _gla_kernel(*refs, mode, t_blk, sub, n_levels):
    if mode == "gla":
        (q_ref, k_ref, v_ref, g_ref, small_ref, wup_ref, bup_ref, gain_ref, code_ref, o_ref, st_ref) = refs
    else:
        (q_ref, f_ref, v_ref, g_ref, loglb_ref, log1mlb_ref, onemlb_ref, gain_ref, code_ref, o_ref, st_ref) = refs

    @pl.when(pl.program_id(1) == 0)
    def _():
        st_ref[...] = jnp.zeros_like(st_ref)

    scale = DK ** -0.5
    if mode == "gla":
        q = q_ref[...] * scale
        k = k_ref[...]
        z = jnp.dot(small_ref[...], wup_ref[...], preferred_element_type=F32,
                    precision=lax.Precision.HIGHEST) + bup_ref[...]
        la = _log_sigmoid(z) / GLA_GATE_NORMALIZER
    else:
        q = _silu(q_ref[...]) * scale
        fp = f_ref[...]
        k = onemlb_ref[...] * _sigmoid(-fp)
        a = loglb_ref[...]
        c = log1mlb_ref[...] + _log_sigmoid(fp)
        la = jnp.maximum(a, c) + jnp.log1p(jnp.exp(-jnp.abs(a - c)))
    v = v_ref[...]
    b = _cumsum_rows(la, t_blk)
    code = code_ref[...]

    scores = jnp.zeros((t_blk, t_blk), F32)
    blk = t_blk
    for lvl in range(n_levels):
        half = blk // 2
        ref_rows = [jnp.broadcast_to(b[j * blk + half:j * blk + half + 1, :], (blk, DK)) for j in range(t_blk // blk)]
        ref = ref_rows[0] if len(ref_rows) == 1 else jnp.concatenate(ref_rows, axis=0)
        qs = q * jnp.exp(jnp.minimum(b - ref, 0.0))
        ks = k * jnp.exp(jnp.minimum(ref - b, 0.0))
        sc = _dot_nt(qs.astype(BF16), ks.astype(BF16))
        scores = jnp.where(code == sub + lvl, sc, scores)
        blk = half
    ones = jnp.ones((DK, t_blk), BF16)
    for d in range(sub):
        if d == 0:
            e = q * k
        else:
            e = q * pltpu.roll(k, d, 0) * jnp.exp(jnp.minimum(b - pltpu.roll(b, d, 0), 0.0))
        rs = _dot(e.astype(BF16), ones)
        scores = jnp.where(code == d, rs, scores)

    st = st_ref[...]
    o = _dot(scores.astype(BF16), v.astype(BF16)) + _dot_nt((q * jnp.exp(b)).astype(BF16), st.astype(BF16))
    b_last = b[t_blk - 1:t_blk, :]
    k_dec = k * jnp.exp(b_last - b)
    st_ref[...] = st * jnp.exp(b_last) + _dot_tn(v.astype(BF16), k_dec.astype(BF16))

    ms = jnp.mean(o * o, axis=-1, keepdims=True)
    o_ref[...] = (o * lax.rsqrt(ms + EPS) * gain_ref[...] * _silu(g_ref[...])).astype(o_ref.dtype)


def gla_mixer(proj, mode, offs, extras, gain, t_blk=GLA_T, sub=GLA_SUB):
    s = proj.shape[0]
    code, n_levels = _gla_codes(t_blk, sub)
    oq, ok, ov, og = offs
    in_specs = [
        pl.BlockSpec((t_blk, DK), lambda h, t: (t, oq // DK + h)),
        pl.BlockSpec((t_blk, DK), lambda h, t: (t, ok // DK + h)),
        pl.BlockSpec((t_blk, DV), lambda h, t: (t, ov // DV + h)),
        pl.BlockSpec((t_blk, DV), lambda h, t: (t, og // DV + h)),
    ]
    args = [proj, proj, proj, proj]
    if mode == "gla":
        small, w_up, b_up = extras
        in_specs += [pl.BlockSpec((t_blk, SMALL_W), lambda h, t: (t, 0)),
                     pl.BlockSpec((SMALL_W, DK), lambda h, t: (0, h)),
                     pl.BlockSpec((1, DK), lambda h, t: (0, h))]
        args += [small, w_up, b_up]
    else:
        vec = pl.BlockSpec((1, DK), lambda h, t: (0, h))
        in_specs += [vec, vec, vec]
        args += list(extras)
    in_specs += [pl.BlockSpec((1, DV), lambda h, t: (0, 0)), pl.BlockSpec((t_blk, t_blk), lambda h, t: (0, 0))]
    args += [gain.reshape(1, DV), jnp.asarray(code)]
    return pl.pallas_call(
        functools.partial(_gla_kernel, mode=mode, t_blk=t_blk, sub=sub, n_levels=n_levels),
        grid=(N_HEADS, s // t_blk), in_specs=in_specs,
        out_specs=pl.BlockSpec((t_blk, DV), lambda h, t: (t, h)),
        out_shape=jax.ShapeDtypeStruct((s, GROUP_WIDTH), BF16),
        scratch_shapes=[pltpu.VMEM((DV, DK), F32)],
        compiler_params=_params(("arbitrary", "arbitrary")), name="gla_" + mode,
    )(*args)


ML_T = 128


def _mlstm_kernel(q_ref, k_ref, v_ref, og_ref, small_ref, smallt_ref, bi_ref, bf_ref, gain_ref, o_ref,
                  c_ref, n_ref, m_ref, *, t_blk):
    h = pl.program_id(0)

    @pl.when(pl.program_id(1) == 0)
    def _():
        c_ref[...] = jnp.zeros_like(c_ref)
        n_ref[...] = jnp.zeros_like(n_ref)
        m_ref[...] = jnp.zeros_like(m_ref)

    b_i = bi_ref[h]
    b_f = bf_ref[h]
    sm = small_ref[...]
    lane = lax.broadcasted_iota(jnp.int32, sm.shape, 1)
    i_col = jnp.sum(jnp.where(lane == SM_ML_I + h, sm, 0.0), axis=1, keepdims=True) + b_i
    f_col = jnp.sum(jnp.where(lane == SM_ML_F + h, sm, 0.0), axis=1, keepdims=True) + b_f
    i_row = smallt_ref[pl.ds(h, 1), :] + b_i
    f_row = smallt_ref[pl.ds(N_HEADS + h, 1), :] + b_f
    lf_col = _log_sigmoid(f_col)
    lf_row = _log_sigmoid(f_row)

    rowi = lax.broadcasted_iota(jnp.int32, (t_blk, t_blk), 0)
    coli = lax.broadcasted_iota(jnp.int32, (t_blk, t_blk), 1)
    causal = coli <= rowi
    b_col = jnp.sum(jnp.where(causal, lf_row, 0.0), axis=1, keepdims=True)
    b_row = jnp.sum(jnp.where(rowi <= coli, lf_col, 0.0), axis=0, keepdims=True)
    neg = jnp.float32(-jnp.inf)
    dlog = jnp.where(causal, b_col - b_row + i_row, neg)
    m_prev = m_ref[...]
    g = b_col + m_prev
    m_t = jnp.maximum(g, jnp.max(dlog, axis=1, keepdims=True))
    w = jnp.exp(dlog - m_t)
    inter = jnp.exp(g - m_t)

    q = q_ref[...] * (DK ** -0.5)
    k = k_ref[...]
    v = v_ref[...]
    qb = q.astype(BF16)
    vb = v.astype(BF16)
    c_st = c_ref[...]
    n_st = n_ref[...]
    qk = _dot_nt(qb, k.astype(BF16)) * w
    num = _dot(qk.astype(BF16), vb) + inter * _dot(qb, c_st.astype(BF16))
    den = jnp.sum(qk, axis=1, keepdims=True) + inter * jnp.sum(q * n_st, axis=1, keepdims=True)
    hid = num / jnp.maximum(jnp.abs(den), jnp.exp(-m_t))

    b_last = b_col[t_blk - 1:t_blk, :]
    g_last = b_last + m_prev
    d_last = b_last - b_col + i_col
    m_new = jnp.maximum(g_last, jnp.max(d_last, axis=0, keepdims=True))
    ws = jnp.exp(d_last - m_new)
    sc = jnp.exp(g_last - m_new)
    kw = k * ws
    c_ref[...] = sc * c_st + _dot_tn(kw.astype(BF16), vb)
    n_ref[...] = sc * n_st + jnp.sum(kw, axis=0, keepdims=True)
    m_ref[...] = m_new

    ms = jnp.mean(hid * hid, axis=-1, keepdims=True)
    o_ref[...] = (hid * lax.rsqrt(ms + EPS) * gain_ref[...] * _sigmoid(og_ref[...])).astype(o_ref.dtype)


def mlstm_mixer(proj, small, small_t, b_i, b_f, gain, t_blk=ML_T):
    s = proj.shape[0]
    smem = pl.BlockSpec(memory_space=pltpu.SMEM)
    return pl.pallas_call(
        functools.partial(_mlstm_kernel, t_blk=t_blk), grid=(N_HEADS, s // t_blk),
        in_specs=[
            pl.BlockSpec((t_blk, DK), lambda h, t: (t, OFF_ML_Q // DK + h)),
            pl.BlockSpec((t_blk, DK), lambda h, t: (t, OFF_ML_K // DK + h)),
            pl.BlockSpec((t_blk, DV), lambda h, t: (t, OFF_ML_V // DV + h)),
            pl.BlockSpec((t_blk, DV), lambda h, t: (t, OFF_ML_O // DV + h)),
            pl.BlockSpec((t_blk, SMALL_W), lambda h, t: (t, 0)),
            pl.BlockSpec((8, t_blk), lambda h, t: (SM_ML_I // 8, t)),
            smem, smem,
            pl.BlockSpec((1, DV), lambda h, t: (0, h)),
        ],
        out_specs=pl.BlockSpec((t_blk, DV), lambda h, t: (t, h)),
        out_shape=jax.ShapeDtypeStruct((s, GROUP_WIDTH), BF16),
        scratch_shapes=[pltpu.VMEM((DK, DV), F32), pltpu.VMEM((1, DK), F32), pltpu.VMEM((1, 1), F32)],
        compiler_params=_params(("arbitrary", "arbitrary")), name="mlstm",
    )(proj, proj, proj, proj, small, small_t, b_i, b_f, gain.reshape(1, GROUP_WIDTH))


RG_T = 128
RG_HALO = 8


def _rglru_kernel(x_ref, gate_ref, cw_ref, cb_ref, wa_ref, ba_ref, wx_ref, bx_ref, lam_ref, o_ref,
                  xbuf_ref, h_ref, *, t_blk):
    @pl.when(pl.program_id(0) == 0)
    def _():
        xbuf_ref[0:RG_HALO, :] = jnp.zeros((RG_HALO, GROUP_WIDTH), F32)
        h_ref[...] = jnp.zeros_like(h_ref)

    x = x_ref[...]
    xbuf_ref[RG_HALO:RG_HALO + t_blk, :] = x
    xc = cb_ref[...] + cw_ref[RG_CONV - 1:RG_CONV, :] * x
    for j in range(RG_CONV - 1):
        xc = xc + cw_ref[j:j + 1, :] * xbuf_ref[pl.ds(RG_HALO - (RG_CONV - 1) + j, t_blk), :]
    xbuf_ref[0:RG_HALO, :] = x[t_blk - RG_HALO:t_blk, :]

    xcb = xc.astype(BF16)
    ra, rx = [], []
    for n in range(RG_BLOCKS):
        xn = xcb[:, n * RG_BLOCK:(n + 1) * RG_BLOCK]
        ra.append(_dot(xn, wa_ref[n]))
        rx.append(_dot(xn, wx_ref[n]))
    r = _sigmoid(jnp.concatenate(ra, axis=1) + ba_ref[...])
    i = _sigmoid(jnp.concatenate(rx, axis=1) + bx_ref[...])
    nlam = -lam_ref[...]
    softplus = jnp.maximum(nlam, 0.0) + jnp.log1p(jnp.exp(-jnp.abs(nlam)))
    log_a = -RG_C * r * softplus
    a = jnp.exp(log_a)
    u = jnp.sqrt(1.0 - jnp.exp(2.0 * log_a)) * (i * xc)

    row = lax.broadcasted_iota(jnp.int32, a.shape, 0)
    s = 1
    while s < t_blk:
        keep = row >= s
        a_sh = jnp.where(keep, pltpu.roll(a, s, 0), 1.0)
        u_sh = jnp.where(keep, pltpu.roll(u, s, 0), 0.0)
        u = a * u_sh + u
        a = a * a_sh
        s *= 2
    hid = u + a * h_ref[...]
    h_ref[...] = hid[t_blk - 1:t_blk, :]
    o_ref[...] = (jax.nn.gelu(gate_ref[...]) * hid).astype(o_ref.dtype)


def rglru_mixer(proj, conv_w, conv_b, w_a, b_a, w_x, b_x, lam, t_blk=RG_T):
    s = proj.shape[0]
    vec = pl.BlockSpec((1, GROUP_WIDTH), lambda t: (0, 0))
    wspec = pl.BlockSpec((RG_BLOCKS, RG_BLOCK, RG_BLOCK), lambda t: (0, 0, 0))
    row = lambda a: a.reshape(1, GROUP_WIDTH)
    return pl.pallas_call(
        functools.partial(_rglru_kernel, t_blk=t_blk), grid=(s // t_blk,),
        in_specs=[
            pl.BlockSpec((t_blk, GROUP_WIDTH), lambda t: (t, OFF_RG_X // GROUP_WIDTH)),
            pl.BlockSpec((t_blk, GROUP_WIDTH), lambda t: (t, OFF_RG_GATE // GROUP_WIDTH)),
            pl.BlockSpec((RG_CONV, GROUP_WIDTH), lambda t: (0, 0)), vec, wspec, vec, wspec, vec, vec,
        ],
        out_specs=pl.BlockSpec((t_blk, GROUP_WIDTH), lambda t: (t, 0)),
        out_shape=jax.ShapeDtypeStruct((s, GROUP_WIDTH), BF16),
        scratch_shapes=[pltpu.VMEM((RG_HALO + t_blk, GROUP_WIDTH), F32), pltpu.VMEM((1, GROUP_WIDTH), F32)],
        compiler_params=_params(("arbitrary",)), name="rglru",
    )(proj, proj, conv_w, row(conv_b), w_a.astype(BF16), row(b_a), w_x.astype(BF16), row(b_x), row(lam))


def _pad_cols(w, n):
    return jnp.pad(w, ((0, 0), (0, n - w.shape[1])))


def _pad_rows(w, n):
    return jnp.pad(w, ((0, n - w.shape[0]), (0, 0)))


def kernel(x, p, attn_norm, w_in, w_out, gla_w_up, gla_b_up, gla_norm, rg_conv_w, rg_conv_b, rg_w_a, rg_b_a,
           rg_w_x, rg_b_x, rg_lambda, ml_b_i, ml_b_f, ml_norm, hg_lb_logits, hg_norm, ffn_norm, ffn_w1, ffn_w3,
           ffn_w2, moe_router, moe_w1, moe_w3, moe_w2, ple_norm, ple_w_gate, ple_w_proj, final_norm):
    bsz, seq, d = x.shape
    s = bsz * seq
    h = x.reshape(s, d)

    sm = jax.nn.softmax(hg_lb_logits.astype(F32), axis=0)
    lb_all = jnp.cumsum(jnp.where(jnp.arange(DEPTH)[:, None] > 0, sm, 0.0), axis=0)

    for l in range(DEPTH):
        hn = rmsnorm(h, attn_norm[l], BF16)
        wl = w_in[l]
        w_main = jnp.concatenate([wl[:, 0:3072], wl[:, 3088:8208], wl[:, 8216:11288]], axis=1).astype(BF16)
        w_small = _pad_cols(jnp.concatenate([wl[:, 3072:3088], wl[:, 8208:8216]], axis=1), SMALL_W).astype(BF16)
        proj = mm_plain(hn, w_main, F32)
        small, small_t = small_proj(hn, w_small)

        w_up = _pad_rows(gla_w_up[l], SMALL_W)
        y_a = gla_mixer(proj, "gla", (OFF_GA_Q, OFF_GA_K, OFF_GA_V, OFF_GA_G),
                        (small, w_up, gla_b_up[l].reshape(1, -1)), gla_norm[l])
        y_b = rglru_mixer(proj, rg_conv_w[l], rg_conv_b[l], rg_w_a[l], rg_b_a[l], rg_w_x[l], rg_b_x[l],
                          rg_lambda[l])
        y_c = mlstm_mixer(proj, small, small_t, ml_b_i[l], ml_b_f[l], ml_norm[l])
        lb = lb_all[l].reshape(1, -1)
        y_d = gla_mixer(proj, "hgrn", (OFF_HG_Q, OFF_HG_F, OFF_HG_I, OFF_HG_G),
                        (jnp.log(lb), jnp.log1p(-lb), 1.0 - lb), hg_norm[l])
        h = mm_out((y_a, y_b, y_c, y_d), w_out[l].astype(BF16), h)

        j = l // 2
        if l % 2 == 0:
            hn = rmsnorm(h, ffn_norm[l], BF16)
            act = swiglu_up(hn, _pad_cols(ffn_w1[j].astype(BF16), D_FF_PAD), _pad_cols(ffn_w3[j].astype(BF16), D_FF_PAD))
            h = mm_res(act, _pad_rows(ffn_w2[j].astype(BF16), D_FF_PAD), h)
        else:
            hn, comb = rmsnorm_router(h, ffn_norm[l], moe_router[j])
            comb_t = jnp.transpose(comb[:, :N_EXPERTS]).reshape(N_EXPERTS, s, 1)
            for e in range(N_EXPERTS):
                act = swiglu_up(hn, _pad_cols(moe_w1[j, e].astype(BF16), D_FF_EXPERT_PAD),
                                _pad_cols(moe_w3[j, e].astype(BF16), D_FF_EXPERT_PAD))
                h = mm_res(act, _pad_rows(moe_w2[j, e].astype(BF16), D_FF_EXPERT_PAD), h, scale=comb_t[e])

        hn = rmsnorm(h, ple_norm[l], BF16)
        h = ple(hn, ple_w_gate[l].astype(BF16), p[l].reshape(s, PLE_DIM), ple_w_proj[l].astype(BF16), h)

    return rmsnorm(h, final_norm, F32).reshape(bsz, seq, d)
```
